```python
import jax, jax.numpy as jnp
from jax import lax
import numpy as np

D_MODEL = 2048
BATCH = 8
SEQ = 2048
DEPTH = 4

MOBA_WIDTH = D_MODEL // 2
MOBA_HEAD_DIM = 64
MOBA_HEADS = MOBA_WIDTH // MOBA_HEAD_DIM
MOBA_BLOCK = 256
MOBA_TOPK = 3
MOBA_Q_CHUNK = 16
ROPE_THETA = 500000.0
ROPE_DIM = MOBA_HEAD_DIM // 4
GMLP_WIDTH = D_MODEL - MOBA_WIDTH
GMLP_CHUNK = 128
GMLP_GROUP_DIM = 128
GMLP_GROUPS = GMLP_WIDTH // GMLP_GROUP_DIM
GMLP_LN_EPS = 1e-5
EVEN_IN_WIDTH = 3 * MOBA_WIDTH + 2 * GMLP_WIDTH

RWKV_HEAD_DIM = 64
RWKV_HEADS = D_MODEL // RWKV_HEAD_DIM
DECAY_LORA = 96
AAA_LORA = 96
MV_LORA = 64
GATE_LORA = 256
RWKV_GN_EPS = 64e-5

D_FF = ((8 * D_MODEL // 3 + 255) // 256) * 256
CONV_WIDTH = 3
NORM_EPS = 1e-6

N_EVEN = (DEPTH + 1) // 2
N_ODD = DEPTH // 2
N_VRES = max(N_ODD - 1, 0)

kernel_name = 'hybrid_moba_gmlp_rwkv7_convffn'


def rms_norm(x, g):
    xf = x.astype(jnp.float32)
    y = xf * lax.rsqrt(jnp.mean(xf * xf, axis=-1, keepdims=True) + NORM_EPS)
    return (y * g.astype(jnp.float32)).astype(x.dtype)


def rope_tables(seq, dtype):
    inv = jnp.power(ROPE_THETA, -jnp.arange(0, ROPE_DIM, 2, dtype=jnp.float32) / ROPE_DIM)
    ang = jnp.arange(seq, dtype=jnp.float32)[:, None] * inv[None, :]
    return jnp.cos(ang).astype(dtype), jnp.sin(ang).astype(dtype)


def partial_rotary(t, cos, sin):
    half = ROPE_DIM // 2
    t1 = t[..., :half]
    t2 = t[..., half:ROPE_DIM]
    return jnp.concatenate([t1 * cos - t2 * sin, t1 * sin + t2 * cos, t[..., ROPE_DIM:]], axis=-1)


def moba_attention(q, k, v):
    B, H, S, Dh = q.shape
    nb = -(-S // MOBA_BLOCK)
    pad = nb * MOBA_BLOCK - S
    kp = jnp.pad(k, ((0, 0), (0, 0), (0, pad), (0, 0)))
    vp = jnp.pad(v, ((0, 0), (0, 0), (0, pad), (0, 0)))
    kb = kp.reshape(B, H, nb, MOBA_BLOCK, Dh)
    vb = vp.reshape(B, H, nb, MOBA_BLOCK, Dh)
    k_mean = jnp.mean(kb.astype(jnp.float32), axis=3).astype(q.dtype)
    n_sel = min(MOBA_TOPK, nb)
    scale = Dh ** -0.5
    bi = jnp.arange(B)[:, None, None, None]
    hi = jnp.arange(H)[None, :, None, None]

    def one_chunk(c):
        start = c * MOBA_Q_CHUNK
        blk = start // MOBA_BLOCK
        qc = lax.dynamic_slice_in_dim(q, start, MOBA_Q_CHUNK, axis=2)
        gate = jnp.einsum('bhqd,bhnd->bhqn', qc, k_mean).astype(jnp.float32)
        gate = jnp.where(jnp.arange(nb) < blk, gate, -jnp.inf)
        _, sel = lax.top_k(gate, n_sel)
        sel_ok = jnp.arange(n_sel) < blk
        k_sel = kb[bi, hi, sel]
        v_sel = vb[bi, hi, sel]
        s_sel = jnp.einsum('bhqd,bhqnkd->bhqnk', qc, k_sel).astype(jnp.float32) * scale
        s_sel = jnp.where(sel_ok[:, None], s_sel, -jnp.inf)
        k_own = lax.dynamic_slice_in_dim(kp, blk * MOBA_BLOCK, MOBA_BLOCK, axis=2)
        v_own = lax.dynamic_slice_in_dim(vp, blk * MOBA_BLOCK, MOBA_BLOCK, axis=2)
        s_own = jnp.einsum('bhqd,bhkd->bhqk', qc, k_own).astype(jnp.float32) * scale
        q_pos = start + jnp.arange(MOBA_Q_CHUNK)
        k_pos = blk * MOBA_BLOCK + jnp.arange(MOBA_BLOCK)
        s_own = jnp.where(k_pos[None, :] <= q_pos[:, None], s_own, -jnp.inf)
        scores = jnp.concatenate([s_sel.reshape(B, H, MOBA_Q_CHUNK, n_sel * MOBA_BLOCK), s_own], axis=-1)
        p = jax.nn.softmax(scores, axis=-1).astype(v.dtype)
        p_sel = p[..., :n_sel * MOBA_BLOCK].reshape(B, H, MOBA_Q_CHUNK, n_sel, MOBA_BLOCK)
        p_own = p[..., n_sel * MOBA_BLOCK:]
        return (jnp.einsum('bhqnk,bhqnkd->bhqd', p_sel, v_sel)
                + jnp.einsum('bhqk,bhkd->bhqd', p_own, v_own))

    out = lax.map(one_chunk, jnp.arange(S // MOBA_Q_CHUNK))
    return out.transpose(1, 2, 0, 3, 4).reshape(B, H, S, Dh)


def chunked_sgu(u, v, ln_g, ln_b, w_s, b_s):
    B, S, _ = v.shape
    nc = S // GMLP_CHUNK
    vg = v.reshape(B, nc, GMLP_CHUNK, GMLP_GROUPS, GMLP_GROUP_DIM).astype(jnp.float32)
    mu = jnp.mean(vg, axis=-1, keepdims=True)
    var = jnp.mean(jnp.square(vg - mu), axis=-1, keepdims=True)
    vn = (vg - mu) * lax.rsqrt(var + GMLP_LN_EPS)
    vn = (vn * ln_g.reshape(GMLP_GROUPS, GMLP_GROUP_DIM).astype(jnp.float32)
          + ln_b.reshape(GMLP_GROUPS, GMLP_GROUP_DIM).astype(jnp.float32)).astype(v.dtype)
    causal = jnp.tril(jnp.ones((GMLP_CHUNK, GMLP_CHUNK), dtype=bool))
    w = jnp.where(causal[None], w_s, 0.0)
    mixed = jnp.einsum('gts,bcsgd->bctgd', w, vn) + b_s.T[:, :, None]
    return u * mixed.reshape(B, S, GMLP_WIDTH).astype(u.dtype)


def even_mixer(h, w_in, w_out, ln_g, ln_b, w_s, b_s, cos, sin):
    B, S, _ = h.shape
    z = h @ w_in
    q, k, v, u_g, v_g = jnp.split(
        z, [MOBA_WIDTH, 2 * MOBA_WIDTH, 3 * MOBA_WIDTH, 3 * MOBA_WIDTH + GMLP_WIDTH], axis=-1)

    def heads(t):
        return t.reshape(B, S, MOBA_HEADS, MOBA_HEAD_DIM).transpose(0, 2, 1, 3)

    att = moba_attention(partial_rotary(heads(q), cos, sin), partial_rotary(heads(k), cos, sin), heads(v))
    att = att.transpose(0, 2, 1, 3).reshape(B, S, MOBA_WIDTH)
    sgu = chunked_sgu(jax.nn.gelu(u_g, approximate=False), jax.nn.gelu(v_g, approximate=False),
                      ln_g, ln_b, w_s, b_s)
    return jnp.concatenate([att, sgu], axis=-1) @ w_out


def rwkv7_time_mix(h, mu, w_rkv, w_o, w0, w1, w2, a0, a1, a2, g1, g2, k_k, k_a, r_k, gn_g, gn_b,
                   v_first, vres):
    B, S, D = h.shape
    H, N = RWKV_HEADS, RWKV_HEAD_DIM
    f32 = jnp.float32
    h_prev = jnp.pad(h, ((0, 0), (1, 0), (0, 0)))[:, :-1]
    dx = h_prev - h
    xr = h + dx * mu[0]
    xw = h + dx * mu[1]
    xk = h + dx * mu[2]
    xv = h + dx * mu[3]
    xa = h + dx * mu[4]
    xg = h + dx * mu[5]
    r = xr @ w_rkv[0]
    k = xk @ w_rkv[1]
    v = xv @ w_rkv[2]
    w_log = -jax.nn.softplus(-(w0 + jnp.tanh(xw @ w1) @ w2).astype(f32)) - 0.5
    decay = jnp.exp(-jnp.exp(w_log))
    a = jax.nn.sigmoid((a0 + (xa @ a1) @ a2).astype(f32))
    g = jax.nn.sigmoid(xg @ g1) @ g2
    if vres is None:
        v_first = v
    else:
        v0, v1, v2 = vres
        v = v + (v_first - v) * jax.nn.sigmoid(v0 + (xv @ v1) @ v2)
    kk = (k * k_k).astype(f32).reshape(B, S, H, N)
    kk = kk / jnp.maximum(jnp.sqrt(jnp.sum(kk * kk, axis=-1, keepdims=True)), 1e-12)
    k = k.astype(f32) * (1.0 + (a - 1.0) * k_a.astype(f32))

    r4 = r.astype(f32).reshape(B, S, H, N)
    k4 = k.reshape(B, S, H, N)
    v4 = v.astype(f32).reshape(B, S, H, N)
    w4 = decay.reshape(B, S, H, N)
    a4 = a.reshape(B, S, H, N)

    def tmaj(t):
        return t.transpose(1, 0, 2, 3)

    def step(state, inp):
        r_t, w_t, k_t, v_t, kk_t, a_t = inp
        sa = jnp.einsum('bhvk,bhk->bhv', state, -kk_t)
        state = (state * w_t[:, :, None, :] + sa[..., None] * (kk_t * a_t)[:, :, None, :]
                 + v_t[..., None] * k_t[:, :, None, :])
        return state, jnp.einsum('bhvk,bhk->bhv', state, r_t)

    s0 = jnp.zeros((B, H, N, N), f32)
    _, o = lax.scan(step, s0, (tmaj(r4), tmaj(w4), tmaj(k4), tmaj(v4), tmaj(kk), tmaj(a4)))
    o = o.transpose(1, 0, 2, 3)
    m = jnp.mean(o, axis=-1, keepdims=True)
    var = jnp.mean(jnp.square(o - m), axis=-1, keepdims=True)
    o = ((o - m) * lax.rsqrt(var + RWKV_GN_EPS) * gn_g.astype(f32).reshape(H, N)
         + gn_b.astype(f32).reshape(H, N))
    bonus = jnp.sum(r4 * k4 * r_k.astype(f32).reshape(H, N), axis=-1, keepdims=True) * v4
    o = (o + bonus).reshape(B, S, D).astype(h.dtype)
    return (o * g) @ w_o, v_first


def conv_ffn(h, w_up, conv_w, conv_b, w_down):
    S = h.shape[1]
    gate, up = jnp.split(h @ w_up, 2, axis=-1)
    gp = jnp.pad(gate, ((0, 0), (CONV_WIDTH - 1, 0), (0, 0)))
    conv = conv_b + gp[:, 0:S] * conv_w[0]
    for j in range(1, CONV_WIDTH):
        conv = conv + gp[:, j:j + S] * conv_w[j]
    return (jax.nn.silu(conv) * up) @ w_down


def setup_inputs(seed: int = 0) -> dict:
    key = jax.random.key(seed)
    ks = iter(jax.random.split(key, 40))
    f32 = jnp.float32
    D = D_MODEL

    def nrm(shape, scale):
        return jax.random.normal(next(ks), shape, f32) * scale

    def gain(shape):
        return 1.0 + nrm(shape, 0.02)

    return {
        'x': nrm((BATCH, SEQ, D), 1.0),
        'mix_norm_g': gain((DEPTH, D)),
        'ffn_norm_g': gain((DEPTH, D)),
        'final_norm_g': gain((D,)),
        'even_w_in': nrm((N_EVEN, D, EVEN_IN_WIDTH), D ** -0.5),
        'even_w_out': nrm((N_EVEN, MOBA_WIDTH + GMLP_WIDTH, D), (MOBA_WIDTH + GMLP_WIDTH) ** -0.5),
        'sgu_ln_g': gain((N_EVEN, GMLP_WIDTH)),
        'sgu_ln_b': nrm((N_EVEN, GMLP_WIDTH), 0.02),
        'sgu_w': nrm((N_EVEN, GMLP_GROUPS, GMLP_CHUNK, GMLP_CHUNK), GMLP_CHUNK ** -0.5),
        'sgu_b': nrm((N_EVEN, GMLP_GROUPS, GMLP_CHUNK), 0.02),
        'rwkv_mu': jax.random.uniform(next(ks), (N_ODD, 6, D), f32),
        'rwkv_w_rkv': nrm((N_ODD, 3, D, D), D ** -0.5),
        'rwkv_w_o': nrm((N_ODD, D, D), D ** -0.5),
        'rwkv_w0': jnp.linspace(-6.0, -1.0, D, dtype=f32)[None, :] + nrm((N_ODD, D), 0.1),
        'rwkv_w1': nrm((N_ODD, D, DECAY_LORA), D ** -0.5),
        'rwkv_w2': nrm((N_ODD, DECAY_LORA, D), 0.1 * DECAY_LORA ** -0.5),
        'rwkv_a0': nrm((N_ODD, D), 0.1),
        'rwkv_a1': nrm((N_ODD, D, AAA_LORA), D ** -0.5),
        'rwkv_a2': nrm((N_ODD, AAA_LORA, D), 0.5 * AAA_LORA ** -0.5),
        'rwkv_g1': nrm((N_ODD, D, GATE_LORA), D ** -0.5),
        'rwkv_g2': nrm((N_ODD, GATE_LORA, D), GATE_LORA ** -0.5),
        'rwkv_k_k': 0.85 + nrm((N_ODD, D), 0.02),
        'rwkv_k_a': gain((N_ODD, D)),
        'rwkv_r_k': nrm((N_ODD, D), 0.1),
        'rwkv_gn_g': gain((N_ODD, D)),
        'rwkv_gn_b': nrm((N_ODD, D), 0.02),
        'rwkv_v0': nrm((N_VRES, D), 0.1),
        'rwkv_v1': nrm((N_VRES, D, MV_LORA), D ** -0.5),
        'rwkv_v2': nrm((N_VRES, MV_LORA, D), 0.5 * MV_LORA ** -0.5),
        'ffn_w_up': nrm((DEPTH, D, 2 * D_FF), D ** -0.5),
        'ffn_conv_w': nrm((DEPTH, CONV_WIDTH, D_FF), CONV_WIDTH ** -0.5),
        'ffn_conv_b': nrm((DEPTH, D_FF), 0.02),
        'ffn_w_down': nrm((DEPTH, D_FF, D), D_FF ** -0.5),
    }


def reference(x, mix_norm_g, ffn_norm_g, final_norm_g,
              even_w_in, even_w_out, sgu_ln_g, sgu_ln_b, sgu_w, sgu_b,
              rwkv_mu, rwkv_w_rkv, rwkv_w_o, rwkv_w0, rwkv_w1, rwkv_w2,
              rwkv_a0, rwkv_a1, rwkv_a2, rwkv_g1, rwkv_g2, rwkv_k_k, rwkv_k_a, rwkv_r_k,
              rwkv_gn_g, rwkv_gn_b, rwkv_v0, rwkv_v1, rwkv_v2,
              ffn_w_up, ffn_conv_w, ffn_conv_b, ffn_w_down):
    S = x.shape[1]
    cos, sin = rope_tables(S, x.dtype)
    v_first = None
    for layer in range(DEPTH):
        i = layer // 2
        h = rms_norm(x, mix_norm_g[layer])
        if layer % 2 == 0:
            y = even_mixer(h, even_w_in[i], even_w_out[i], sgu_ln_g[i], sgu_ln_b[i],
                           sgu_w[i], sgu_b[i], cos, sin)
        else:
            vres = None if v_first is None else (rwkv_v0[i - 1], rwkv_v1[i - 1], rwkv_v2[i - 1])
            y, v_first = rwkv7_time_mix(h, rwkv_mu[i], rwkv_w_rkv[i], rwkv_w_o[i], rwkv_w0[i],
                                        rwkv_w1[i], rwkv_w2[i], rwkv_a0[i], rwkv_a1[i], rwkv_a2[i],
                                        rwkv_g1[i], rwkv_g2[i], rwkv_k_k[i], rwkv_k_a[i], rwkv_r_k[i],
                                        rwkv_gn_g[i], rwkv_gn_b[i], v_first, vres)
        x = x + y
        h = rms_norm(x, ffn_norm_g[layer])
        x = x + conv_ffn(h, ffn_w_up[layer], ffn_conv_w[layer], ffn_conv_b[layer], ffn_w_down[layer])
    return rms_norm(x, final_norm_g)
```

```python
import functools
import math

import jax
import jax.numpy as jnp
from jax import lax
from jax.experimental import pallas as pl
from jax.experimental.pallas import tpu as pltpu

F32 = jnp.float32
BF16 = jnp.bfloat16

MOBA_WIDTH = 1024
MOBA_HEAD_DIM = 64
MOBA_BLOCK = 256
MOBA_TOPK = 3
ROPE_THETA = 500000.0
ROPE_DIM = 16
GMLP_WIDTH = 1024
GMLP_CHUNK = 128
GMLP_GROUP_DIM = 128
GMLP_LN_EPS = 1e-5
RWKV_HEAD_DIM = 64
RWKV_GN_EPS = 64e-5
CONV_WIDTH = 3
NORM_EPS = 1e-6

LANES = 128
SUBLANES = 8
VMEM_LIMIT = 56 * 1024 * 1024

RWKV_CHUNK = 64
RWKV_CHUNKS_PER_STEP = 2


def _params(*sem):
    return pltpu.CompilerParams(dimension_semantics=sem, vmem_limit_bytes=VMEM_LIMIT)


def _bdot(a, b):
    return jnp.dot(a.astype(BF16), b.astype(BF16), preferred_element_type=F32)


def _split(x):
    hi = x.astype(BF16)
    lo = (x - hi.astype(F32)).astype(BF16)
    return hi, lo


def _dot3(a, b, dims=(((1,), (0,)), ((), ()))):
    ah, al = _split(a)
    bh, bl = _split(b)
    dg = functools.partial(lax.dot_general, dimension_numbers=dims, preferred_element_type=F32)
    return dg(ah, bh) + (dg(ah, bl) + dg(al, bh))


def _rms_norm(x, g):
    ms = jnp.mean(x * x, axis=-1, keepdims=True)
    return x * lax.rsqrt(ms + NORM_EPS) * g


def _gelu(x):
    return 0.5 * x * (1.0 + lax.erf(x * math.sqrt(0.5)))


def _norm_mm_kernel(*refs, epilogue):
    if epilogue == "rope":
        x_ref, g_ref, w_ref, c_ref, s1_ref, s2_ref, o_ref, h_ref = refs
    else:
        x_ref, g_ref, w_ref, o_ref, h_ref = refs

    @pl.when(pl.program_id(1) == 0)
    def _():
        h_ref[...] = _rms_norm(x_ref[...], g_ref[...]).astype(BF16)

    z = jnp.dot(h_ref[...], w_ref[...], preferred_element_type=F32)
    if epilogue == "rope":
        tn = z.shape[1]
        z = (z * c_ref[...] + pltpu.roll(z, tn - ROPE_DIM // 2, 1) * s1_ref[...]
             + pltpu.roll(z, ROPE_DIM // 2, 1) * s2_ref[...])
    elif epilogue == "gelu":
        z = _gelu(z)
    o_ref[...] = z.astype(o_ref.dtype)


def norm_matmul(x, g, w, *, epilogue="none", rope=None, seq=None, out_dtype=F32, tm=512, tn=512):
    M, D = x.shape
    N = w.shape[1]
    assert M % tm == 0 and N % tn == 0
    in_specs = [
        pl.BlockSpec((tm, D), lambda i, j: (i, 0)),
        pl.BlockSpec((1, D), lambda i, j: (0, 0)),
        pl.BlockSpec((D, tn), lambda i, j: (0, j)),
    ]
    args = [x, g.reshape(1, D), w]
    if epilogue == "rope":
        per_seq = seq // tm
        for t in rope:
            assert t.shape == (seq, tn)
            in_specs.append(pl.BlockSpec((tm, tn), lambda i, j: (i % per_seq, 0)))
            args.append(t)
    return pl.pallas_call(
        functools.partial(_norm_mm_kernel, epilogue=epilogue),
        grid=(M // tm, N // tn),
        in_specs=in_specs,
        out_specs=pl.BlockSpec((tm, tn), lambda i, j: (i, j)),
        out_shape=jax.ShapeDtypeStruct((M, N), out_dtype),
        scratch_shapes=[pltpu.VMEM((tm, D), BF16)],
        compiler_params=_params("parallel", "arbitrary"),
        name="norm_matmul_" + epilogue,
    )(*args)


def _mm_res_kernel(*refs, n_pairs):
    lhs = refs[:n_pairs]
    ws = refs[n_pairs:2 * n_pairs]
    res_ref = refs[2 * n_pairs]
    o_ref = refs[2 * n_pairs + 1]
    acc = res_ref[...]
    for a_ref, w_ref in zip(lhs, ws):
        acc = acc + jnp.dot(a_ref[...], w_ref[...], preferred_element_type=F32)
    o_ref[...] = acc


def matmul_residual(lhs_list, w_list, res, *, tm=512, tn=512):
    M, N = res.shape
    n = len(lhs_list)
    in_specs = [pl.BlockSpec((tm, a.shape[1]), lambda i, j: (i, 0)) for a in lhs_list]
    in_specs += [pl.BlockSpec((w.shape[0], tn), lambda i, j: (0, j)) for w in w_list]
    in_specs.append(pl.BlockSpec((tm, tn), lambda i, j: (i, j)))
    return pl.pallas_call(
        functools.partial(_mm_res_kernel, n_pairs=n),
        grid=(M // tm, N // tn),
        in_specs=in_specs,
        out_specs=pl.BlockSpec((tm, tn), lambda i, j: (i, j)),
        out_shape=jax.ShapeDtypeStruct((M, N), F32),
        compiler_params=_params("parallel", "parallel"),
        name="matmul_residual",
    )(*lhs_list, *w_list, res)


def _moba_kernel(q_ref, k_ref, vt_ref, o_ref, kmean_ref, sel_ref, *, nb):
    blk = MOBA_BLOCK
    hd = MOBA_HEAD_DIM
    qb = pl.program_id(2)

    @pl.when(qb == 0)
    def _():
        for n in range(nb):
            kmean_ref[n:n + 1, :] = jnp.mean(k_ref[n].astype(F32), axis=0, keepdims=True)

    q2 = q_ref[...].astype(F32) * (hd ** -0.5)
    lane = lax.broadcasted_iota(jnp.int32, (blk, LANES), 1)
    row_n = lax.broadcasted_iota(jnp.int32, (nb, blk), 0)
    key_i = lax.broadcasted_iota(jnp.int32, (blk, blk), 0)
    qry_i = lax.broadcasted_iota(jnp.int32, (blk, blk), 1)
    nt_dims = (((1,), (1,)), ((), ()))
    outs = []
    for a in range(2):
        head_mask = (lane < hd) if a == 0 else (lane >= hd)
        qa = jnp.where(head_mask, q2, 0.0)
        qa_bf = qa.astype(BF16)
        gt = lax.dot_general(kmean_ref[...], qa, nt_dims, precision=lax.Precision.HIGHEST,
                             preferred_element_type=F32)
        rank = jnp.zeros((nb, blk), F32)
        for m in range(nb):
            gm = gt[m:m + 1, :]
            beats = jnp.where(gm > gt, 1.0, jnp.where((gm == gt) & (row_n > m), 1.0, 0.0))
            rank = rank + jnp.where(m < qb, beats, 0.0)
        sel_ref[a] = jnp.where((rank < MOBA_TOPK) & (row_n < qb), 1.0, 0.0)

        s = lax.dot_general(k_ref[qb], qa_bf, nt_dims, preferred_element_type=F32)
        s = jnp.where(key_i <= qry_i, s, -jnp.inf)
        m0 = jnp.max(s, axis=0, keepdims=True)
        p = jnp.exp(s - m0)
        l0 = jnp.sum(p, axis=0, keepdims=True)
        acc0 = jnp.dot(vt_ref[qb, a * hd:(a + 1) * hd, :], p.astype(BF16), preferred_element_type=F32)

        def past_block(n, carry, a=a, qa_bf=qa_bf):
            m_i, l_i, acc = carry
            s = lax.dot_general(k_ref[n], qa_bf, nt_dims, preferred_element_type=F32)
            ok = sel_ref[a, pl.ds(n, 1), :] > 0.5
            s = jnp.where(ok, s, -jnp.inf)
            m_new = jnp.maximum(m_i, jnp.max(s, axis=0, keepdims=True))
            alpha = jnp.exp(m_i - m_new)
            p = jnp.exp(s - m_new)
            l_new = alpha * l_i + jnp.sum(p, axis=0, keepdims=True)
            pv = jnp.dot(vt_ref[n, a * hd:(a + 1) * hd, :], p.astype(BF16), preferred_element_type=F32)
            return m_new, l_new, alpha * acc + pv

        _, l_f, acc_f = lax.fori_loop(0, qb, past_block, (m0, l0, acc0))
        outs.append(acc_f / l_f)
    o_ref[...] = jnp.concatenate(outs, axis=0).T.astype(o_ref.dtype)


def moba_attention(q, k, vt, *, batch, seq):
    nb = seq // MOBA_BLOCK
    W = q.shape[1]
    pairs = W // LANES
    return pl.pallas_call(
        functools.partial(_moba_kernel, nb=nb),
        grid=(batch, pairs, nb),
        in_specs=[
            pl.BlockSpec((MOBA_BLOCK, LANES), lambda b, h, i: (b * nb + i, h)),
            pl.BlockSpec((None, nb, MOBA_BLOCK, LANES), lambda b, h, i: (b, 0, 0, h)),
            pl.BlockSpec((None, nb, LANES, MOBA_BLOCK), lambda b, h, i: (b, 0, h, 0)),
        ],
        out_specs=pl.BlockSpec((MOBA_BLOCK, LANES), lambda b, h, i: (b * nb + i, h)),
        out_shape=jax.ShapeDtypeStruct((batch * seq, W), BF16),
        scratch_shapes=[pltpu.VMEM((nb, LANES), F32), pltpu.VMEM((2, nb, MOBA_BLOCK), F32)],
        compiler_params=_params("parallel", "parallel", "arbitrary"),
        name="moba_attention",
    )(q, k, vt)


def _sgu_kernel(u_ref, v_ref, lng_ref, lnb_ref, w_ref, bt_ref, o_ref, *, chunks, groups):
    T = GMLP_CHUNK
    dg = GMLP_GROUP_DIM
    r_i = lax.broadcasted_iota(jnp.int32, (T, T), 0)
    c_i = lax.broadcasted_iota(jnp.int32, (T, T), 1)
    for g in range(groups):
        w = jnp.where(c_i <= r_i, w_ref[g], 0.0).astype(BF16)
        bias = bt_ref[:, g:g + 1]
        ln_g = lng_ref[:, g * dg:(g + 1) * dg]
        ln_b = lnb_ref[:, g * dg:(g + 1) * dg]
        for c in range(chunks):
            rows = slice(c * T, (c + 1) * T)
            cols = slice(g * dg, (g + 1) * dg)
            v = v_ref[rows, cols]
            mu = jnp.mean(v, axis=-1, keepdims=True)
            d = v - mu
            var = jnp.mean(d * d, axis=-1, keepdims=True)
            vn = (d * lax.rsqrt(var + GMLP_LN_EPS) * ln_g + ln_b).astype(BF16)
            mixed = jnp.dot(w, vn, preferred_element_type=F32) + bias
            o_ref[rows, cols] = (u_ref[rows, cols] * mixed).astype(o_ref.dtype)


def chunked_sgu(u, v, ln_g, ln_b, w_s, b_s, *, chunks=4):
    M, W = u.shape
    groups = W // GMLP_GROUP_DIM
    tm = chunks * GMLP_CHUNK
    return pl.pallas_call(
        functools.partial(_sgu_kernel, chunks=chunks, groups=groups),
        grid=(M // tm,),
        in_specs=[
            pl.BlockSpec((tm, W), lambda i: (i, 0)),
            pl.BlockSpec((tm, W), lambda i: (i, 0)),
            pl.BlockSpec((1, W), lambda i: (0, 0)),
            pl.BlockSpec((1, W), lambda i: (0, 0)),
            pl.BlockSpec((groups, GMLP_CHUNK, GMLP_CHUNK), lambda i: (0, 0, 0)),
            pl.BlockSpec((GMLP_CHUNK, groups), lambda i: (0, 0)),
        ],
        out_specs=pl.BlockSpec((tm, W), lambda i: (i, 0)),
        out_shape=jax.ShapeDtypeStruct((M, W), BF16),
        compiler_params=_params("parallel"),
        name="chunked_sgu",
    )(u, v, ln_g.reshape(1, W), ln_b.reshape(1, W), w_s, b_s.T)


def _ffn_up_kernel(x_ref, g_ref, wg_ref, wu_ref, cw_ref, cb_ref, o_ref, h_ref, gs_ref, tail_ref, *,
                   tiles_per_seq):
    i = pl.program_id(0)
    j = pl.program_id(1)
    tm = x_ref.shape[0]

    @pl.when(j == 0)
    def _():
        h_ref[...] = _rms_norm(x_ref[...], g_ref[...]).astype(BF16)

    h = h_ref[...]
    gate = jnp.dot(h, wg_ref[...], preferred_element_type=F32)
    up = jnp.dot(h, wu_ref[...], preferred_element_type=F32)

    first = (i % tiles_per_seq) == 0

    @pl.when(first)
    def _():
        gs_ref[0:SUBLANES, :] = jnp.zeros((SUBLANES, gs_ref.shape[1]), F32)

    @pl.when(jnp.logical_not(first))
    def _():
        gs_ref[0:SUBLANES, :] = tail_ref[j]

    gs_ref[SUBLANES:, :] = gate
    tail_ref[j] = gate[tm - SUBLANES:, :]
    conv = cb_ref[...]
    for t in range(CONV_WIDTH):
        off = SUBLANES - (CONV_WIDTH - 1) + t
        conv = conv + gs_ref[off:off + tm, :] * cw_ref[t:t + 1, :]
    o_ref[...] = (conv * jax.nn.sigmoid(conv) * up).astype(o_ref.dtype)


def ffn_up(x, g, w_up, conv_w, conv_b, *, seq, tm=512, tf=512):
    M, D = x.shape
    F = w_up.shape[1] // 2
    nf = F // tf
    return pl.pallas_call(
        functools.partial(_ffn_up_kernel, tiles_per_seq=seq // tm),
        grid=(M // tm, nf),
        in_specs=[
            pl.BlockSpec((tm, D), lambda i, j: (i, 0)),
            pl.BlockSpec((1, D), lambda i, j: (0, 0)),
            pl.BlockSpec((D, tf), lambda i, j: (0, j)),
            pl.BlockSpec((D, tf), lambda i, j: (0, j + nf)),
            pl.BlockSpec((CONV_WIDTH, tf), lambda i, j: (0, j)),
            pl.BlockSpec((1, tf), lambda i, j: (0, j)),
        ],
        out_specs=pl.BlockSpec((tm, tf), lambda i, j: (i, j)),
        out_shape=jax.ShapeDtypeStruct((M, F), BF16),
        scratch_shapes=[
            pltpu.VMEM((tm, D), BF16),
            pltpu.VMEM((tm + SUBLANES, tf), F32),
            pltpu.VMEM((nf, SUBLANES, tf), F32),
        ],
        compiler_params=_params("arbitrary", "arbitrary"),
        name="ffn_up",
    )(x, g.reshape(1, D), w_up, w_up, conv_w, conv_b.reshape(1, F))


def _rwkv_mix_kernel(x_ref, g_ref, mu_ref, *rest, tiles_per_seq):
    outs = rest[:6]
    hs_ref, tail_ref = rest[6:]
    i = pl.program_id(0)
    tm = x_ref.shape[0]
    h = _rms_norm(x_ref[...], g_ref[...])
    first = (i % tiles_per_seq) == 0

    @pl.when(first)
    def _():
        hs_ref[0:SUBLANES, :] = jnp.zeros((SUBLANES, hs_ref.shape[1]), F32)

    @pl.when(jnp.logical_not(first))
    def _():
        hs_ref[0:SUBLANES, :] = tail_ref[...]

    hs_ref[SUBLANES:, :] = h
    tail_ref[...] = h[tm - SUBLANES:, :]
    dx = hs_ref[SUBLANES - 1:SUBLANES - 1 + tm, :] - h
    for n, o_ref in enumerate(outs):
        o_ref[...] = (h + dx * mu_ref[n:n + 1, :]).astype(o_ref.dtype)


def rwkv_mix(x, g, mu, *, seq, tm=256):
    M, D = x.shape
    spec = pl.BlockSpec((tm, D), lambda i: (i, 0))
    return pl.pallas_call(
        functools.partial(_rwkv_mix_kernel, tiles_per_seq=seq // tm),
        grid=(M // tm,),
        in_specs=[spec, pl.BlockSpec((1, D), lambda i: (0, 0)), pl.BlockSpec((6, D), lambda i: (0, 0))],
        out_specs=[spec] * 6,
        out_shape=[jax.ShapeDtypeStruct((M, D), BF16)] * 6,
        scratch_shapes=[pltpu.VMEM((tm + SUBLANES, D), F32), pltpu.VMEM((SUBLANES, D), F32)],
        compiler_params=_params("arbitrary"),
        name="rwkv_mix",
    )(x, g.reshape(1, D), mu)


def _mm_kernel(a_ref, w_ref, o_ref):
    o_ref[...] = jnp.dot(a_ref[...], w_ref[...], preferred_element_type=F32).astype(o_ref.dtype)


def matmul(a, w, *, out_dtype=F32, tm=512, tn=512):
    M, K = a.shape
    N = w.shape[1]
    return pl.pallas_call(
        _mm_kernel,
        grid=(M // tm, N // tn),
        in_specs=[pl.BlockSpec((tm, K), lambda i, j: (i, 0)), pl.BlockSpec((K, tn), lambda i, j: (0, j))],
        out_specs=pl.BlockSpec((tm, tn), lambda i, j: (i, j)),
        out_shape=jax.ShapeDtypeStruct((M, N), out_dtype),
        compiler_params=_params("parallel", "parallel"),
        name="matmul",
    )(a, w)


def _lora_kernel(*refs, kind):
    if kind == "vres":
        x_ref, w1_ref, w2_ref, b_ref, v_ref, vf_ref, o_ref = refs
    elif kind == "gate":
        x_ref, w1_ref, w2_ref, o_ref = refs
    else:
        x_ref, w1_ref, w2_ref, b_ref, o_ref = refs
    mid = jnp.dot(x_ref[...], w1_ref[...], preferred_element_type=F32)
    if kind == "decay":
        mid = jnp.tanh(mid)
    elif kind == "gate":
        mid = jax.nn.sigmoid(mid)
    y = jnp.dot(mid.astype(BF16), w2_ref[...], preferred_element_type=F32)
    if kind == "decay":
        o_ref[...] = -jax.nn.sigmoid(b_ref[...] + y) * math.exp(-0.5)
    elif kind == "aaa":
        o_ref[...] = jax.nn.sigmoid(b_ref[...] + y)
    elif kind == "gate":
        o_ref[...] = y
    else:
        v = v_ref[...]
        o_ref[...] = v + (vf_ref[...] - v) * jax.nn.sigmoid(b_ref[...] + y)


def lora(x, w1, w2, bias=None, *, kind, extra=(), tm=512):
    M, D = x.shape
    R = w1.shape[1]
    rp = -(-R // LANES) * LANES
    w1p = jnp.pad(w1, ((0, 0), (0, rp - R))).astype(BF16)
    w2p = jnp.pad(w2, ((0, rp - R), (0, 0))).astype(BF16)
    row = pl.BlockSpec((tm, D), lambda i: (i, 0))
    in_specs = [row, pl.BlockSpec((D, rp), lambda i: (0, 0)), pl.BlockSpec((rp, D), lambda i: (0, 0))]
    args = [x, w1p, w2p]
    if bias is not None:
        in_specs.append(pl.BlockSpec((1, D), lambda i: (0, 0)))
        args.append(bias.reshape(1, D))
    for e in extra:
        in_specs.append(row)
        args.append(e)
    return pl.pallas_call(
        functools.partial(_lora_kernel, kind=kind),
        grid=(M // tm,),
        in_specs=in_specs,
        out_specs=row,
        out_shape=jax.ShapeDtypeStruct((M, D), F32),
        compiler_params=_params("parallel"),
        name="lora_" + kind,
    )(*args)


def _seg_sum(x, lane_lo):
    s_lo = jnp.sum(jnp.where(lane_lo, x, 0.0), axis=-1, keepdims=True)
    s_hi = jnp.sum(jnp.where(lane_lo, 0.0, x), axis=-1, keepdims=True)
    return jnp.where(lane_lo, s_lo, s_hi)


def _rwkv_core_kernel(r_ref, lw_ref, k_ref, v_ref, a_ref, g_ref, kk_ref, ka_ref, rk_ref, gng_ref, gnb_ref,
                      o_ref, state_ref, *, chunks):
    C = RWKV_CHUNK
    hd = RWKV_HEAD_DIM

    @pl.when(pl.program_id(2) == 0)
    def _():
        state_ref[...] = jnp.zeros_like(state_ref)

    lane_lo = lax.broadcasted_iota(jnp.int32, (C, LANES), 1) < hd
    t_i = lax.broadcasted_iota(jnp.int32, (C, C), 0)
    s_i = lax.broadcasted_iota(jnp.int32, (C, C), 1)
    tri_incl = jnp.where(s_i <= t_i, 1.0, 0.0)
    eye = jnp.where(s_i == t_i, 1.0, 0.0)
    bd_r = lax.broadcasted_iota(jnp.int32, (LANES, LANES), 0) < hd
    bd_c = lax.broadcasted_iota(jnp.int32, (LANES, LANES), 1) < hd
    block_diag = bd_r == bd_c
    eye_l = (lax.broadcasted_iota(jnp.int32, (LANES, LANES), 0)
             == lax.broadcasted_iota(jnp.int32, (LANES, LANES), 1))
    nt = (((1,), (1,)), ((), ()))
    tn = (((0,), (0,)), ((), ()))

    per_chunk = []
    for c in range(chunks):
        rows = slice(c * C, (c + 1) * C)
        r = r_ref[rows, :]
        lw = lw_ref[rows, :]
        k = k_ref[rows, :]
        v = v_ref[rows, :]
        a = a_ref[rows, :]
        kk = k * kk_ref[...]
        nrm = jnp.sqrt(_seg_sum(kk * kk, lane_lo))
        kk = kk / jnp.maximum(nrm, 1e-12)
        km = k * (1.0 + (a - 1.0) * ka_ref[...])
        bonus = _seg_sum(r * km * rk_ref[...], lane_lo) * v
        at = -kk
        b = kk * a

        cum = jnp.dot(tri_incl, lw, precision=lax.Precision.HIGHEST, preferred_element_type=F32)
        g_incl = jnp.exp(cum)
        g_excl = jnp.exp(cum - lw)
        g_inv = jnp.exp(-cum)
        g_last = g_incl[C - 1:C, :]
        a_g = at * g_excl
        r_g = r * g_incl
        b_i = b * g_inv
        k_i = km * g_inv
        b_dec = b_i * g_last
        k_dec = k_i * g_last

        rhs_bk = jnp.concatenate([b_i, k_i], axis=0)
        p_parts, q_parts, y_parts, z_parts = [], [], [], []
        for h in range(2):
            hm = lane_lo if h == 0 else jnp.logical_not(lane_lo)
            lhs = jnp.concatenate([jnp.where(hm, a_g, 0.0), jnp.where(hm, r_g, 0.0)], axis=0)
            big = _dot3(lhs, rhs_bk, nt)
            a_ab = jnp.where(s_i < t_i, big[:C, :C], 0.0)
            a_ak = jnp.where(s_i < t_i, big[:C, C:], 0.0)
            a_rb = jnp.where(s_i <= t_i, big[C:, :C], 0.0)
            a_rk = jnp.where(s_i <= t_i, big[C:, C:], 0.0)
            t_inv = eye + a_ab
            pw = a_ab
            span = 2
            while span < C:
                pw = _dot3(pw, pw)
                t_inv = t_inv + _dot3(pw, t_inv)
                span *= 2
            x_rhs = jnp.where(hm, a_g, 0.0)
            p_h = _dot3(t_inv, x_rhs)
            q_h = _dot3(t_inv, _dot3(a_ak, v))
            p_parts.append(p_h)
            q_parts.append(q_h)
            y_parts.append(_dot3(a_rb, p_h))
            z_parts.append((a_rb, a_rk))
        p2 = p_parts[0] + p_parts[1]
        q2 = jnp.where(lane_lo, q_parts[0], q_parts[1])
        y2 = r_g + y_parts[0] + y_parts[1]
        qv = jnp.concatenate([q2, v], axis=0)
        z_h = [_dot3(jnp.concatenate(z_parts[h], axis=1), qv) for h in range(2)]
        z2 = jnp.where(lane_lo, z_h[0], z_h[1])
        m_mat = jnp.where(block_diag, _dot3(b_dec, p2, tn), 0.0)
        n_mat = jnp.where(block_diag, _dot3(jnp.concatenate([b_dec, k_dec], axis=0), qv, tn), 0.0)
        g_col = jnp.sum(jnp.where(eye_l, g_last, 0.0), axis=1, keepdims=True)
        per_chunk.append((y2, z2, m_mat, g_col, n_mat, bonus))

    state = state_ref[...]
    for c in range(chunks):
        y2, z2, m_mat, g_col, n_mat, bonus = per_chunk[c]
        rows = slice(c * C, (c + 1) * C)
        o = _dot3(y2, state) + z2
        state = g_col * state + _dot3(m_mat, state) + n_mat
        mean = _seg_sum(o, lane_lo) * (1.0 / hd)
        d = o - mean
        var = _seg_sum(d * d, lane_lo) * (1.0 / hd)
        o = d * lax.rsqrt(var + RWKV_GN_EPS) * gng_ref[...] + gnb_ref[...]
        o_ref[rows, :] = ((o + bonus) * g_ref[rows, :]).astype(o_ref.dtype)
    state_ref[...] = state


def rwkv_core(r, lw, k, v, a, g, k_k, k_a, r_k, gn_g, gn_b, *, batch, seq):
    M, D = r.shape
    rows = RWKV_CHUNK * RWKV_CHUNKS_PER_STEP
    steps = seq // rows
    pairs = D // LANES
    blk = pl.BlockSpec((rows, LANES), lambda b, h, c: (b * steps + c, h))
    par = pl.BlockSpec((1, LANES), lambda b, h, c: (0, h))
    return pl.pallas_call(
        functools.partial(_rwkv_core_kernel, chunks=RWKV_CHUNKS_PER_STEP),
        grid=(batch, pairs, steps),
        in_specs=[blk] * 6 + [par] * 5,
        out_specs=blk,
        out_shape=jax.ShapeDtypeStruct((M, D), BF16),
        scratch_shapes=[pltpu.VMEM((LANES, LANES), F32)],
        compiler_params=_params("parallel", "parallel", "arbitrary"),
        name="rwkv_core",
    )(r, lw, k, v, a, g, *(p.reshape(1, D) for p in (k_k, k_a, r_k, gn_g, gn_b)))


def _final_norm_kernel(x_ref, g_ref, o_ref):
    o_ref[...] = _rms_norm(x_ref[...], g_ref[...])


def final_norm(x, g, *, tm=512):
    M, D = x.shape
    spec = pl.BlockSpec((tm, D), lambda i: (i, 0))
    return pl.pallas_call(
        _final_norm_kernel,
        grid=(M // tm,),
        in_specs=[spec, pl.BlockSpec((1, D), lambda i: (0, 0))],
        out_specs=spec,
        out_shape=jax.ShapeDtypeStruct((M, D), F32),
        compiler_params=_params("parallel"),
        name="final_norm",
    )(x, g.reshape(1, D))


def _rope_tables(seq, width):
    half = ROPE_DIM // 2
    inv = jnp.power(ROPE_THETA, -jnp.arange(0, ROPE_DIM, 2, dtype=F32) / ROPE_DIM)
    ang = jnp.arange(seq, dtype=F32)[:, None] * inv[None, :]
    cos, sin = jnp.cos(ang), jnp.sin(ang)
    ones = jnp.ones((seq, MOBA_HEAD_DIM - ROPE_DIM), F32)
    zeros = jnp.zeros((seq, MOBA_HEAD_DIM - ROPE_DIM), F32)
    zh = jnp.zeros((seq, half), F32)
    c = jnp.concatenate([cos, cos, ones], axis=1)
    s1 = jnp.concatenate([-sin, zh, zeros], axis=1)
    s2 = jnp.concatenate([zh, sin, zeros], axis=1)
    reps = width // MOBA_HEAD_DIM
    return tuple(jnp.tile(t, (1, reps)) for t in (c, s1, s2))


def _even_layer(x, norm_g, w_in, w_out, ln_g, ln_b, w_s, b_s, rope, *, batch, seq):
    W = MOBA_WIDTH
    w_in = w_in.astype(BF16)
    qk = norm_matmul(x, norm_g, w_in[:, :2 * W], epilogue="rope", rope=rope, seq=seq, out_dtype=BF16)
    v = norm_matmul(x, norm_g, w_in[:, 2 * W:3 * W], out_dtype=BF16)
    uv = norm_matmul(x, norm_g, w_in[:, 3 * W:], epilogue="gelu")
    nb = seq // MOBA_BLOCK
    q = qk[:, :W]
    k = qk[:, W:].reshape(batch, nb, MOBA_BLOCK, W)
    vt = v.reshape(batch, nb, MOBA_BLOCK, W).transpose(0, 1, 3, 2)
    att = moba_attention(q, k, vt, batch=batch, seq=seq)
    sgu = chunked_sgu(uv[:, :GMLP_WIDTH], uv[:, GMLP_WIDTH:], ln_g, ln_b, w_s, b_s)
    w_out = w_out.astype(BF16)
    return matmul_residual([att, sgu], [w_out[:W], w_out[W:]], x)


def _odd_layer(x, norm_g, mu, w_rkv, w_o, w0, w1, w2, a0, a1, a2, g1, g2, k_k, k_a, r_k, gn_g, gn_b,
               v_first, vres, *, batch, seq):
    xr, xw, xk, xv, xa, xg = rwkv_mix(x, norm_g, mu, seq=seq)
    w_rkv = w_rkv.astype(BF16)
    r = matmul(xr, w_rkv[0])
    k = matmul(xk, w_rkv[1])
    v = matmul(xv, w_rkv[2])
    lw = lora(xw, w1, w2, w0, kind="decay")
    a = lora(xa, a1, a2, a0, kind="aaa")
    g = lora(xg, g1, g2, kind="gate")
    if vres is None:
        v_first = v
    else:
        v0, v1, v2 = vres
        v = lora(xv, v1, v2, v0, kind="vres", extra=(v, v_first))
    o = rwkv_core(r, lw, k, v, a, g, k_k, k_a, r_k, gn_g, gn_b, batch=batch, seq=seq)
    return matmul_residual([o], [w_o.astype(BF16)], x), v_first


def _conv_ffn(x, norm_g, w_up, conv_w, conv_b, w_down, *, seq):
    act = ffn_up(x, norm_g, w_up.astype(BF16), conv_w, conv_b, seq=seq)
    return matmul_residual([act], [w_down.astype(BF16)], x, tm=1024, tn=512)


def kernel(x, mix_norm_g, ffn_norm_g, final_norm_g, even_w_in, even_w_out, sgu_ln_g, sgu_ln_b, sgu_w, sgu_b, rwkv_mu, rwkv_w_rkv, rwkv_w_o, rwkv_w0, rwkv_w1, rwkv_w2, rwkv_a0, rwkv_a1, rwkv_a2, rwkv_g1, rwkv_g2, rwkv_k_k, rwkv_k_a, rwkv_r_k, rwkv_gn_g, rwkv_gn_b, rwkv_v0, rwkv_v1, rwkv_v2, ffn_w_up, ffn_conv_w, ffn_conv_b, ffn_w_down):
    batch, seq, d_model = x.shape
    depth = mix_norm_g.shape[0]
    rope = _rope_tables(seq, 512)
    xs = x.reshape(batch * seq, d_model)
    v_first = None
    for layer in range(depth):
        i = layer // 2
        if layer % 2 == 0:
            xs = _even_layer(xs, mix_norm_g[layer], even_w_in[i], even_w_out[i], sgu_ln_g[i], sgu_ln_b[i],
                             sgu_w[i], sgu_b[i], rope, batch=batch, seq=seq)
        else:
            vres = None if v_first is None else (rwkv_v0[i - 1], rwkv_v1[i - 1], rwkv_v2[i - 1])
            xs, v_first = _odd_layer(xs, mix_norm_g[layer], rwkv_mu[i], rwkv_w_rkv[i], rwkv_w_o[i],
                                     rwkv_w0[i], rwkv_w1[i], rwkv_w2[i], rwkv_a0[i], rwkv_a1[i], rwkv_a2[i],
                                     rwkv_g1[i], rwkv_g2[i], rwkv_k_k[i], rwkv_k_a[i], rwkv_r_k[i],
                                     rwkv_gn_g[i], rwkv_gn_b[i], v_first, vres, batch=batch, seq=seq)
        xs = _conv_ffn(xs, ffn_norm_g[layer], ffn_w_up[layer], ffn_conv_w[layer], ffn_conv_b[layer],
                       ffn_w_down[layer], seq=seq)
    return final_norm(xs, final_norm_g).reshape(batch, seq, d_model)
```
